```python
import math
import jax
import jax.numpy as jnp
from jax import lax
import numpy as np

D_MODEL = 2048
BATCH = 16
SEQ = 256
DEPTH = 4
DEC_BATCH = 4
DEC_SEQ = 4096
PAST_LEN = 256

GRID_W = 64
H_A = 4
DA_QK = 64
DA_V = 128
H_B = 6
DB_K = 64
DB_V = 128
GATE_RANK = 16
GATE_TAU = 16.0
GLA_CHUNK = 64
H_C = 6
Q_RANK = 384
KV_RANK = 256
DC_NOPE = 64
DC_ROPE = 32
DC_V = 128
MIX_A = H_A * DA_V
MIX_B = H_B * DB_V
MIX_C = H_C * DC_V
MIX_WIDTH = MIX_A + MIX_B + MIX_C
IN_SIZES = (2 * H_A * DA_QK, 2 * H_A * DA_QK, H_A * DA_V, H_B * DB_K, H_B * DB_K, H_B * DB_V, H_B * DB_V, 2 * GATE_RANK, Q_RANK, KV_RANK, DC_ROPE)
IN_COLS = sum(IN_SIZES)
D_FF = 5632
ROPE_BASE = 10000.0
Q_BLOCK = 128
EPS = 1e-6

kernel_name = 'hybrid_diffusion_prefix_trunk_step'


def _rmsnorm(x, g):
    xf = x.astype(jnp.float32)
    y = xf * lax.rsqrt(jnp.mean(xf * xf, axis=-1, keepdims=True) + EPS)
    return (y * g.astype(jnp.float32)).astype(x.dtype)


def _split_cols(p):
    out = []
    start = 0
    for size in IN_SIZES:
        out.append(p[..., start:start + size])
        start += size
    return out


def _rope_tables(n_rows, rot_dim):
    pos_r = jnp.repeat(jnp.arange(n_rows, dtype=jnp.float32), GRID_W)
    pos_c = jnp.tile(jnp.arange(GRID_W, dtype=jnp.float32), n_rows)
    n_freq = rot_dim // 4
    inv = ROPE_BASE ** (-jnp.arange(n_freq, dtype=jnp.float32) / n_freq)
    a_r = pos_r[:, None] * inv[None]
    a_c = pos_c[:, None] * inv[None]
    ang = jnp.concatenate([a_r, a_r, a_c, a_c], axis=-1)
    return jnp.cos(ang), jnp.sin(ang)


def _apply_rope(x, cos, sin):
    shp = (1, cos.shape[0]) + (1,) * (x.ndim - 3) + (cos.shape[1],)
    c = cos.reshape(shp).astype(x.dtype)
    s = sin.reshape(shp).astype(x.dtype)
    x0, x1, x2, x3 = jnp.split(x, 4, axis=-1)
    rot = jnp.concatenate([-x1, x0, -x3, x2], axis=-1)
    return x * c + rot * s


def _block_attend(q_c, k_c, v, scale, q_l=None, k_l=None):
    bsz, n, heads, _ = q_c.shape
    nb = n // Q_BLOCK

    def to_blocks(t):
        return jnp.moveaxis(t.reshape(bsz, nb, Q_BLOCK, heads, t.shape[-1]), 1, 0)

    qs = (to_blocks(q_c),) if q_l is None else (to_blocks(q_c), to_blocks(q_l))

    def one_block(qb):
        logits = jnp.einsum('bqhd,bkhd->bhqk', qb[0], k_c).astype(jnp.float32)
        if q_l is not None:
            logits_l = jnp.einsum('bqhd,bkhd->bhqk', qb[1], k_l).astype(jnp.float32)
            logits = jnp.concatenate([logits, logits_l], axis=-1)
        p = jax.nn.softmax(logits * scale, axis=-1).astype(v.dtype)
        return jnp.einsum('bhqk,bkhe->bqhe', p, v)

    out = lax.map(one_block, qs)
    return jnp.moveaxis(out, 0, 1).reshape(bsz, n, heads, v.shape[-1])


def _diff_attn(p_q, p_k, p_v, lw, layer, rope, ctx):
    bsz, n, _ = p_q.shape
    q = p_q.reshape(bsz, n, H_A, 2, DA_QK)
    k = p_k.reshape(bsz, n, H_A, 2, DA_QK)
    v = p_v.reshape(bsz, n, H_A, DA_V)
    scale = DA_QK ** -0.5
    if ctx is None:
        o1 = _block_attend(q[..., 0, :], k[..., 0, :], v, scale)
        o2 = _block_attend(q[..., 1, :], k[..., 1, :], v, scale)
        new = (k, v)
    else:
        k_c, v_c = ctx
        cos, sin = rope
        q_r = _apply_rope(q, cos, sin)
        k_r = _apply_rope(k, cos, sin)
        v_all = jnp.concatenate([v_c, v], axis=1)
        o1 = _block_attend(q[..., 0, :], k_c[..., 0, :], v_all, scale, q_r[..., 0, :], k_r[..., 0, :])
        o2 = _block_attend(q[..., 1, :], k_c[..., 1, :], v_all, scale, q_r[..., 1, :], k_r[..., 1, :])
        new = None
    lam_init = 0.8 - 0.6 * math.exp(-0.3 * layer)
    lam = (jnp.exp(jnp.sum(lw['lam_q1'].astype(jnp.float32) * lw['lam_k1'].astype(jnp.float32)))
           - jnp.exp(jnp.sum(lw['lam_q2'].astype(jnp.float32) * lw['lam_k2'].astype(jnp.float32))) + lam_init)
    o = o1 - lam.astype(o1.dtype) * o2
    o = _rmsnorm(o, lw['a_norm_g']) * (1.0 - lam_init)
    return o.reshape(bsz, n, MIX_A), new


def _gla_chunk_scan(q, k, v, logg, s0):
    bsz, n, heads, _ = q.shape
    nc = n // GLA_CHUNK

    def chunks(t):
        return jnp.moveaxis(t.astype(jnp.float32).reshape(bsz, nc, GLA_CHUNK, heads, t.shape[-1]), 1, 0)

    causal = jnp.tril(jnp.ones((GLA_CHUNK, GLA_CHUNK), dtype=bool))[None, :, :, None, None]

    def step(s, inp):
        qc, kc, vc, gc = inp
        b = jnp.cumsum(gc, axis=1)
        o_inter = jnp.einsum('bthk,bhkv->bthv', qc * jnp.exp(b), s)
        dec = jnp.exp(jnp.where(causal, b[:, :, None] - b[:, None, :], -jnp.inf))
        att = jnp.einsum('bthk,bshk,btshk->bths', qc, kc, dec)
        o_intra = jnp.einsum('bths,bshv->bthv', att, vc)
        b_last = b[:, -1]
        s_new = jnp.exp(b_last)[..., None] * s + jnp.einsum('bshk,bshv->bhkv', kc * jnp.exp(b_last[:, None] - b), vc)
        return s_new, o_inter + o_intra

    s_end, o = lax.scan(step, s0, (chunks(q), chunks(k), chunks(v), chunks(logg)))
    return jnp.moveaxis(o, 0, 1).reshape(bsz, n, heads, v.shape[-1]).astype(v.dtype), s_end


def _gla(p_q, p_k, p_v, p_r, p_g, lw, ctx):
    bsz, n, _ = p_q.shape
    q = p_q.reshape(bsz, n, H_B, DB_K) * (DB_K ** -0.5)
    k = p_k.reshape(bsz, n, H_B, DB_K)
    v = p_v.reshape(bsz, n, H_B, DB_V)
    g_lr = p_g.reshape(bsz, n, 2, GATE_RANK)
    pre = jnp.einsum('bsdr,drk->bsdk', g_lr, lw['b_gate_w']) + lw['b_gate_b']
    logg = (jax.nn.log_sigmoid(pre.astype(jnp.float32)) / GATE_TAU).reshape(bsz, n, 2, H_B, DB_K)
    if ctx is None:
        s_f = jnp.zeros((bsz, H_B, DB_K, DB_V), jnp.float32)
        s_b = jnp.zeros((bsz, H_B, DB_K, DB_V), jnp.float32)
    else:
        s_f = ctx[:, 0].astype(jnp.float32)
        s_b = ctx[:, 1].astype(jnp.float32)
    o_f, s_f_end = _gla_chunk_scan(q, k, v, logg[:, :, 0], s_f)
    o_b, s_b_end = _gla_chunk_scan(jnp.flip(q, 1), jnp.flip(k, 1), jnp.flip(v, 1), jnp.flip(logg[:, :, 1], 1), s_b)
    o = _rmsnorm(o_f + jnp.flip(o_b, 1), lw['b_norm_g']) * jax.nn.silu(p_r.reshape(bsz, n, H_B, DB_V))
    new = jnp.stack([s_f_end, s_b_end], axis=1).astype(p_q.dtype) if ctx is None else None
    return o.reshape(bsz, n, MIX_B), new


def _mla(p_qd, p_kvd, p_kr, lw, rope, ctx):
    bsz, n, _ = p_qd.shape
    q = (_rmsnorm(p_qd, lw['c_qnorm_g']) @ lw['c_q_up']).reshape(bsz, n, H_C, DC_NOPE + DC_ROPE)
    c_kv = _rmsnorm(p_kvd, lw['c_kvnorm_g'])
    scale = (DC_NOPE + DC_ROPE) ** -0.5

    def expand(ckv, kr):
        b, t, _ = ckv.shape
        kv = (ckv @ lw['c_kv_up']).reshape(b, t, H_C, DC_NOPE + DC_V)
        k = jnp.concatenate([kv[..., :DC_NOPE], jnp.broadcast_to(kr[:, :, None, :], (b, t, H_C, DC_ROPE))], axis=-1)
        return k, kv[..., DC_NOPE:]

    if ctx is None:
        k, v = expand(c_kv, p_kr)
        o = _block_attend(q, k, v, scale)
        new = (c_kv, p_kr)
    else:
        ckv_c, kr_c = ctx
        k_c, v_c = expand(ckv_c, kr_c)
        cos, sin = rope
        q_l = jnp.concatenate([q[..., :DC_NOPE], _apply_rope(q[..., DC_NOPE:], cos, sin)], axis=-1)
        k_l, v_l = expand(c_kv, _apply_rope(p_kr, cos, sin))
        o = _block_attend(q, k_c, jnp.concatenate([v_c, v_l], axis=1), scale, q_l, k_l)
        new = None
    return o.reshape(bsz, n, MIX_C), new


def _conv_ffn(h, w_up, conv_w, conv_b, w_down):
    g, u = jnp.split(h @ w_up, 2, axis=-1)
    gp = jnp.pad(g, ((0, 0), (1, 1), (0, 0)))
    g = gp[:, :-2] * conv_w[0] + gp[:, 1:-1] * conv_w[1] + gp[:, 2:] * conv_w[2] + conv_b
    return (jax.nn.silu(g) * u) @ w_down


def _layer(x, cond, lw, layer, rope, ctx):
    nb = cond.shape[0]
    mod = (jax.nn.silu(cond) @ lw['ada_w'] + lw['ada_b']).reshape(nb, 1, 6, D_MODEL)
    sh1, sc1, g1, sh2, sc2, g2 = (mod[:, :, i] for i in range(6))
    h = _rmsnorm(x, lw['norm1_g']) * (1 + sc1) + sh1
    aq, ak, av, bq, bk, bv, br, bg, cq, ckv, ckr = _split_cols(h @ lw['w_in'])
    if ctx is None:
        rope_a, rope_c = None, None
        ctx_a, ctx_b, ctx_c = None, None, None
    else:
        rope_a, rope_c = rope
        ctx_a, ctx_b, ctx_c = (ctx[0], ctx[1]), ctx[2], (ctx[3], ctx[4])
    a_out, a_new = _diff_attn(aq, ak, av, lw, layer, rope_a, ctx_a)
    b_out, b_new = _gla(bq, bk, bv, br, bg, lw, ctx_b)
    c_out, c_new = _mla(cq, ckv, ckr, lw, rope_c, ctx_c)
    x = x + g1 * (jnp.concatenate([a_out, b_out, c_out], axis=-1) @ lw['w_out'])
    h2 = _rmsnorm(x, lw['norm2_g']) * (1 + sc2) + sh2
    x = x + g2 * _conv_ffn(h2, lw['w_up'], lw['conv_w'], lw['conv_b'], lw['w_down'])
    new = None if ctx is not None else (a_new[0], a_new[1], b_new, c_new[0], c_new[1])
    return x, new


def setup_inputs(seed: int = 0) -> dict:
    key = jax.random.key(seed)
    ks = jax.random.split(key, 32)

    def nrm(i, shape, scale):
        return jax.random.normal(ks[i], shape, jnp.float32) * scale

    return {
        'x_prompt': nrm(0, (BATCH, SEQ, D_MODEL), 1.0),
        'x_sample': nrm(1, (DEC_BATCH, DEC_SEQ, D_MODEL), 1.0),
        'cache_a_k': nrm(2, (DEC_BATCH, DEPTH, PAST_LEN, H_A, 2, DA_QK), 1.0),
        'cache_a_v': nrm(3, (DEC_BATCH, DEPTH, PAST_LEN, H_A, DA_V), 1.0),
        'state_b': nrm(4, (DEC_BATCH, DEPTH, 2, H_B, DB_K, DB_V), 0.5),
        'cache_c_kv': nrm(5, (DEC_BATCH, DEPTH, PAST_LEN, KV_RANK), 1.0),
        'cache_c_krope': nrm(6, (DEC_BATCH, DEPTH, PAST_LEN, DC_ROPE), 1.0),
        'c': nrm(7, (DEC_BATCH, D_MODEL), 1.0),
        'c_ctx': nrm(8, (D_MODEL,), 1.0),
        'norm1_g': 1.0 + nrm(9, (DEPTH, D_MODEL), 0.05),
        'ada_w': nrm(10, (DEPTH, D_MODEL, 6 * D_MODEL), 0.5 * D_MODEL ** -0.5),
        'ada_b': nrm(11, (DEPTH, 6 * D_MODEL), 0.01),
        'w_in': nrm(12, (DEPTH, D_MODEL, IN_COLS), D_MODEL ** -0.5),
        'lam_q1': nrm(13, (DEPTH, DA_QK), 0.1),
        'lam_k1': nrm(14, (DEPTH, DA_QK), 0.1),
        'lam_q2': nrm(15, (DEPTH, DA_QK), 0.1),
        'lam_k2': nrm(16, (DEPTH, DA_QK), 0.1),
        'a_norm_g': 1.0 + nrm(17, (DEPTH, DA_V), 0.05),
        'b_gate_w': nrm(18, (DEPTH, 2, GATE_RANK, H_B * DB_K), GATE_RANK ** -0.5),
        'b_gate_b': nrm(19, (DEPTH, 2, H_B * DB_K), 0.1),
        'b_norm_g': 1.0 + nrm(20, (DEPTH, DB_V), 0.05),
        'c_qnorm_g': 1.0 + nrm(21, (DEPTH, Q_RANK), 0.05),
        'c_q_up': nrm(22, (DEPTH, Q_RANK, H_C * (DC_NOPE + DC_ROPE)), Q_RANK ** -0.5),
        'c_kvnorm_g': 1.0 + nrm(23, (DEPTH, KV_RANK), 0.05),
        'c_kv_up': nrm(24, (DEPTH, KV_RANK, H_C * (DC_NOPE + DC_V)), KV_RANK ** -0.5),
        'w_out': nrm(25, (DEPTH, MIX_WIDTH, D_MODEL), MIX_WIDTH ** -0.5),
        'norm2_g': 1.0 + nrm(26, (DEPTH, D_MODEL), 0.05),
        'w_up': nrm(27, (DEPTH, D_MODEL, 2 * D_FF), D_MODEL ** -0.5),
        'conv_w': nrm(28, (DEPTH, 3, D_FF), 3.0 ** -0.5),
        'conv_b': nrm(29, (DEPTH, D_FF), 0.01),
        'w_down': nrm(30, (DEPTH, D_FF, D_MODEL), D_FF ** -0.5),
        'final_g': 1.0 + nrm(31, (D_MODEL,), 0.05),
    }


def reference(x_prompt, x_sample, cache_a_k, cache_a_v, state_b, cache_c_kv, cache_c_krope, c, c_ctx,
              norm1_g, ada_w, ada_b, w_in, lam_q1, lam_k1, lam_q2, lam_k2, a_norm_g, b_gate_w, b_gate_b,
              b_norm_g, c_qnorm_g, c_q_up, c_kvnorm_g, c_kv_up, w_out, norm2_g, w_up, conv_w, conv_b,
              w_down, final_g):
    n_lat = x_sample.shape[1]
    n_rows = n_lat // GRID_W
    rope = (_rope_tables(n_rows, DA_QK), _rope_tables(n_rows, DC_ROPE))
    cond_ctx = c_ctx[None]
    xp = x_prompt
    xs = x_sample
    ak_l, av_l, sb_l, ckv_l, kr_l = [], [], [], [], []
    for l in range(DEPTH):
        lw = {
            'norm1_g': norm1_g[l], 'ada_w': ada_w[l], 'ada_b': ada_b[l], 'w_in': w_in[l],
            'lam_q1': lam_q1[l], 'lam_k1': lam_k1[l], 'lam_q2': lam_q2[l], 'lam_k2': lam_k2[l],
            'a_norm_g': a_norm_g[l], 'b_gate_w': b_gate_w[l], 'b_gate_b': b_gate_b[l], 'b_norm_g': b_norm_g[l],
            'c_qnorm_g': c_qnorm_g[l], 'c_q_up': c_q_up[l], 'c_kvnorm_g': c_kvnorm_g[l], 'c_kv_up': c_kv_up[l],
            'w_out': w_out[l], 'norm2_g': norm2_g[l], 'w_up': w_up[l], 'conv_w': conv_w[l],
            'conv_b': conv_b[l], 'w_down': w_down[l],
        }
        xp, (ak, av, sb, ckv, kr) = _layer(xp, cond_ctx, lw, l, None, None)
        ak_l.append(ak)
        av_l.append(av)
        sb_l.append(sb)
        ckv_l.append(ckv)
        kr_l.append(kr)
        ctx_l = (cache_a_k[:, l], cache_a_v[:, l], state_b[:, l], cache_c_kv[:, l], cache_c_krope[:, l])
        xs, _ = _layer(xs, c, lw, l, rope, ctx_l)
    y_prompt = _rmsnorm(xp, final_g)
    y_sample = _rmsnorm(xs, final_g)
    new_a_k = jnp.stack(ak_l, axis=1)
    new_a_v = jnp.stack(av_l, axis=1)
    new_state_b = jnp.stack(sb_l, axis=1)
    new_c_kv = jnp.stack(ckv_l, axis=1)
    new_c_krope = jnp.stack(kr_l, axis=1)
    return (y_prompt, y_sample, new_a_k, new_a_v, new_state_b, new_c_kv, new_c_krope)
```

```python
import functools
import math

import numpy as np
import jax
import jax.numpy as jnp
from jax import lax
from jax.experimental import pallas as pl
from jax.experimental.pallas import tpu as pltpu

F32 = jnp.float32
BF16 = jnp.bfloat16

GRID_W = 64
H_A, DA_QK, DA_V = 4, 64, 128
H_B, DB_K, DB_V = 6, 64, 128
GATE_RANK, GATE_TAU, GLA_CHUNK = 16, 16.0, 64
H_C, Q_RANK, KV_RANK = 6, 384, 256
DC_NOPE, DC_ROPE, DC_V = 64, 32, 128
ROPE_BASE = 10000.0
EPS = 1e-6
LANES = 128
HEAD_W = 128

COL_AQ, COL_AK, COL_AV = 0, 512, 1024
COL_BQ, COL_BK, COL_BV, COL_BR = 1536, 1920, 2304, 3072
COL_CQ, COL_G, COL_CKV = 3840, 4224, 4352
P_COLS = 4608
KR_LANE = 64

VMEM_LIMIT = 56 * 1024 * 1024


def _cparams(sem):
    return pltpu.CompilerParams(dimension_semantics=sem, vmem_limit_bytes=VMEM_LIMIT)


def _dot(a, b):
    return jnp.dot(a, b, preferred_element_type=F32)


def _dot_nt(a, b):
    return lax.dot_general(a, b, (((1,), (1,)), ((), ())), preferred_element_type=F32)


def _dot_tn(a, b):
    return lax.dot_general(a, b, (((0,), (0,)), ((), ())), preferred_element_type=F32)


def _silu(x):
    return x / (1.0 + jnp.exp(-x))


def _ada_kernel(c_ref, w_ref, b_ref, o_ref):
    c = _silu(c_ref[...]).astype(BF16)
    o_ref[...] = _dot(c, w_ref[...].astype(BF16)) + b_ref[...]


def _ada(cond8, ada_w, ada_b):
    depth, d, n = ada_w.shape
    tn = 1024
    return pl.pallas_call(
        _ada_kernel,
        out_shape=jax.ShapeDtypeStruct((depth, 8, n), F32),
        grid=(depth, n // tn),
        in_specs=[
            pl.BlockSpec((8, d), lambda l, j: (0, 0)),
            pl.BlockSpec((None, d, tn), lambda l, j: (l, 0, j)),
            pl.BlockSpec((None, 1, tn), lambda l, j: (l, 0, j)),
        ],
        out_specs=pl.BlockSpec((None, 8, tn), lambda l, j: (l, 0, j)),
        compiler_params=_cparams(("parallel", "parallel")),
        name="ada_mod",
    )(cond8, ada_w, ada_b.reshape(depth, 1, n))


def _rms(x, g):
    ms = jnp.mean(x * x, axis=-1, keepdims=True)
    return x * lax.rsqrt(ms + EPS) * g


def _proj_in_kernel(x_ref, g_ref, sc_ref, sh_ref, w_ref, o_ref, h_ref):
    @pl.when(pl.program_id(1) == 0)
    def _():
        h = _rms(x_ref[...], g_ref[...]) * (1.0 + sc_ref[...]) + sh_ref[...]
        h_ref[...] = h.astype(BF16)

    o_ref[...] = _dot(h_ref[...], w_ref[...])


def _proj_in(x, g, mod5, w, layer, row_fn, tm, tn):
    t, d = x.shape
    n = w.shape[1]

    def mspec(which):
        return pl.BlockSpec((None, None, None, 1, d), lambda i, j: (layer, row_fn(i), which, 0, 0))

    return pl.pallas_call(
        _proj_in_kernel,
        out_shape=jax.ShapeDtypeStruct((t, n), F32),
        grid=(t // tm, n // tn),
        in_specs=[
            pl.BlockSpec((tm, d), lambda i, j: (i, 0)),
            pl.BlockSpec((1, d), lambda i, j: (0, 0)),
            mspec(1), mspec(0),
            pl.BlockSpec((d, tn), lambda i, j: (0, j)),
        ],
        out_specs=pl.BlockSpec((tm, tn), lambda i, j: (i, j)),
        scratch_shapes=[pltpu.VMEM((tm, d), BF16)],
        compiler_params=_cparams(("parallel", "arbitrary")),
        name="proj_in",
    )(x, g, mod5, mod5, w)


def _mla_q_kernel(x_ref, g_ref, w_ref, o_ref):
    h = _rms(x_ref[...], g_ref[...]).astype(BF16)
    o_ref[...] = _dot(h, w_ref[...])


def _mla_q(p, g, w, tm):
    t = p.shape[0]
    n = w.shape[1]
    return pl.pallas_call(
        _mla_q_kernel,
        out_shape=jax.ShapeDtypeStruct((t, n), F32),
        grid=(t // tm,),
        in_specs=[
            pl.BlockSpec((tm, Q_RANK), lambda i: (i, COL_CQ // Q_RANK)),
            pl.BlockSpec((1, Q_RANK), lambda i: (0, 0)),
            pl.BlockSpec((Q_RANK, n), lambda i: (0, 0)),
        ],
        out_specs=pl.BlockSpec((tm, n), lambda i: (i, 0)),
        compiler_params=_cparams(("parallel",)),
        name="mla_q",
    )(p, g, w)


def _mla_kv_kernel(x_ref, g_ref, w_ref, ckv_ref, o_ref):
    c = _rms(x_ref[...], g_ref[...])
    ckv_ref[...] = c
    o_ref[...] = _dot(c.astype(BF16), w_ref[...]).astype(BF16)


def _mla_kv(p, g, w, tm):
    t = p.shape[0]
    n = w.shape[1]
    return pl.pallas_call(
        _mla_kv_kernel,
        out_shape=(jax.ShapeDtypeStruct((t, KV_RANK), F32), jax.ShapeDtypeStruct((t, n), BF16)),
        grid=(t // tm,),
        in_specs=[
            pl.BlockSpec((tm, KV_RANK), lambda i: (i, COL_CKV // KV_RANK)),
            pl.BlockSpec((1, KV_RANK), lambda i: (0, 0)),
            pl.BlockSpec((KV_RANK, n), lambda i: (0, 0)),
        ],
        out_specs=(pl.BlockSpec((tm, KV_RANK), lambda i: (i, 0)), pl.BlockSpec((tm, n), lambda i: (i, 0))),
        compiler_params=_cparams(("parallel",)),
        name="mla_kv",
    )(p, g, w)


def _mla_kvc_kernel(x_ref, w_ref, o_ref):
    o_ref[...] = _dot(x_ref[...].astype(BF16), w_ref[...]).astype(BF16)


def _mla_kvc(c, w, tm):
    t = c.shape[0]
    n = w.shape[1]
    return pl.pallas_call(
        _mla_kvc_kernel,
        out_shape=jax.ShapeDtypeStruct((t, n), BF16),
        grid=(t // tm,),
        in_specs=[
            pl.BlockSpec((tm, KV_RANK), lambda i: (i, 0)),
            pl.BlockSpec((KV_RANK, n), lambda i: (0, 0)),
        ],
        out_specs=pl.BlockSpec((tm, n), lambda i: (i, 0)),
        compiler_params=_cparams(("parallel",)),
        name="mla_kvc",
    )(c, w)


def _swap_groups(x, n):
    lane = lax.broadcasted_iota(jnp.int32, x.shape, x.ndim - 1)
    even = (lane & n) == 0
    return jnp.where(even, pltpu.roll(x, LANES - n, x.ndim - 1), pltpu.roll(x, n, x.ndim - 1))


def _rope(x, cos, sin, n):
    return x * cos + _swap_groups(x, n) * sin


def _softmax_parts(q_c, q_l, kc, kl):
    s_l = _dot_nt(q_l, kl)
    m = jnp.max(s_l, axis=-1, keepdims=True)
    if kc is not None:
        s_c = _dot_nt(q_c, kc)
        m = jnp.maximum(m, jnp.max(s_c, axis=-1, keepdims=True))
        p_c = jnp.exp(s_c - m)
    p_l = jnp.exp(s_l - m)
    den = jnp.sum(p_l, axis=-1, keepdims=True)
    if kc is not None:
        den = den + jnp.sum(p_c, axis=-1, keepdims=True)
    else:
        p_c = None
    return p_c, p_l, 1.0 / den


def _attn_a_kernel(has_ctx, lam_init, *refs):
    if has_ctx:
        (xq_ref, xk_ref, xv_ref, cq_ref, sq_ref, ck_ref, sk_ref, lam_ref, g_ref, kc_ref, vc_ref,
         o_ref, kl_s, vl_s, kc_s, vc_s) = refs
    else:
        (xq_ref, xk_ref, xv_ref, cq_ref, sq_ref, ck_ref, sk_ref, lam_ref, g_ref,
         o_ref, kl_s, vl_s) = refs
        kc_s = vc_s = None
    n = DA_QK // 4

    @pl.when(pl.program_id(2) == 0)
    def _():
        kl_s[...] = _rope(xk_ref[...], ck_ref[...], sk_ref[...], n).astype(BF16)
        vl_s[...] = xv_ref[...].astype(BF16)
        if has_ctx:
            kc_s[...] = kc_ref[...].astype(BF16)
            vc_s[...] = vc_ref[...].astype(BF16)

    scale = DA_QK ** -0.5
    x = xq_ref[...] * scale
    r = _rope(x, cq_ref[...], sq_ref[...], n)
    lo = lax.broadcasted_iota(jnp.int32, x.shape, 1) < DA_QK
    lam4 = lam_ref[...]
    lam = (jnp.exp(jnp.sum(lam4[0:1] * lam4[1:2], axis=-1, keepdims=True))
           - jnp.exp(jnp.sum(lam4[2:3] * lam4[3:4], axis=-1, keepdims=True)) + lam_init)
    kl = kl_s[...]
    kc = kc_s[...] if has_ctx else None

    def half(mask):
        q_c = jnp.where(mask, x, 0.0).astype(BF16) if has_ctx else None
        q_l = jnp.where(mask, r, 0.0).astype(BF16)
        return _softmax_parts(q_c, q_l, kc, kl)

    p1c, p1l, w1 = half(lo)
    p2c, p2l, w2 = half(jnp.logical_not(lo))
    w2 = w2 * lam
    o = _dot((p1l * w1 - p2l * w2).astype(BF16), vl_s[...])
    if has_ctx:
        o = o + _dot((p1c * w1 - p2c * w2).astype(BF16), vc_s[...])
    o_ref[...] = (_rms(o, g_ref[...]) * (1.0 - lam_init)).astype(o_ref.dtype)


def _attn_a(p, tabs, lam4, g, layer, grp, ctx):
    off, nb, s = grp
    tq = min(128, s)
    cq, sq = tabs
    has_ctx = ctx is not None
    lam_init = 0.8 - 0.6 * math.exp(-0.3 * layer)
    qrow = lambda b, h, i: (off // tq + b * (s // tq) + i)
    in_specs = [
        pl.BlockSpec((tq, HEAD_W), lambda b, h, i: (qrow(b, h, i), COL_AQ // HEAD_W + h)),
        pl.BlockSpec((s, HEAD_W), lambda b, h, i: (off // s + b, COL_AK // HEAD_W + h)),
        pl.BlockSpec((s, HEAD_W), lambda b, h, i: (off // s + b, COL_AV // HEAD_W + h)),
        pl.BlockSpec((tq, HEAD_W), lambda b, h, i: (i, 0)),
        pl.BlockSpec((tq, HEAD_W), lambda b, h, i: (i, 0)),
        pl.BlockSpec((s, HEAD_W), lambda b, h, i: (0, 0)),
        pl.BlockSpec((s, HEAD_W), lambda b, h, i: (0, 0)),
        pl.BlockSpec((4, DA_QK), lambda b, h, i: (0, 0)),
        pl.BlockSpec((1, DA_V), lambda b, h, i: (0, 0)),
    ]
    args = [p, p, p, cq, sq, cq, sq, lam4, g]
    scratch = [pltpu.VMEM((s, HEAD_W), BF16), pltpu.VMEM((s, HEAD_W), BF16)]
    if has_ctx:
        ck, cv = ctx
        past = ck.shape[2]
        in_specs += [
            pl.BlockSpec((None, None, past, HEAD_W), lambda b, h, i: (b, layer, 0, h)),
            pl.BlockSpec((None, None, past, HEAD_W), lambda b, h, i: (b, layer, 0, h)),
        ]
        args += [ck, cv]
        scratch += [pltpu.VMEM((past, HEAD_W), BF16), pltpu.VMEM((past, HEAD_W), BF16)]
    return pl.pallas_call(
        functools.partial(_attn_a_kernel, has_ctx, lam_init),
        out_shape=jax.ShapeDtypeStruct((nb * s, H_A * DA_V), BF16),
        grid=(nb, H_A, s // tq),
        in_specs=in_specs,
        out_specs=pl.BlockSpec((tq, HEAD_W), lambda b, h, i: (b * (s // tq) + i, h)),
        scratch_shapes=scratch,
        compiler_params=_cparams(("parallel", "parallel", "arbitrary")),
        name="attn_a_ctx" if has_ctx else "attn_a",
    )(*args)


def _attn_c_kernel(has_ctx, *refs):
    if has_ctx:
        (q_ref, kn_ref, v_ref, kr_ref, cq_ref, sq_ref, ck_ref, sk_ref, knc_ref, vc_ref, krc_ref,
         o_ref, kl_s, kc_s) = refs
    else:
        (q_ref, kn_ref, v_ref, kr_ref, cq_ref, sq_ref, ck_ref, sk_ref, o_ref, kl_s) = refs
        kc_s = None
    n = DC_ROPE // 4

    @pl.when(pl.program_id(2) == 0)
    def _():
        kr = _rope(kr_ref[...], ck_ref[...], sk_ref[...], n)
        kl_s[...] = (kn_ref[...].astype(F32) + kr).astype(BF16)
        if has_ctx:
            kc_s[...] = (knc_ref[...].astype(F32) + krc_ref[...]).astype(BF16)

    scale = (DC_NOPE + DC_ROPE) ** -0.5
    x = q_ref[...] * scale
    q_l = _rope(x, cq_ref[...], sq_ref[...], n).astype(BF16)
    q_c = x.astype(BF16) if has_ctx else None
    p_c, p_l, w = _softmax_parts(q_c, q_l, kc_s[...] if has_ctx else None, kl_s[...])
    o = _dot(p_l.astype(BF16), v_ref[...])
    if has_ctx:
        o = o + _dot(p_c.astype(BF16), vc_ref[...])
    o_ref[...] = (o * w).astype(o_ref.dtype)


def _attn_c(qn, kv, p, tabs, layer, grp, ctx):
    off, nb, s = grp
    tq = min(128, s)
    cq, sq, ck, sk = tabs
    has_ctx = ctx is not None
    qrow = lambda b, h, i: (off // tq + b * (s // tq) + i)
    in_specs = [
        pl.BlockSpec((tq, HEAD_W), lambda b, h, i: (qrow(b, h, i), h)),
        pl.BlockSpec((s, HEAD_W), lambda b, h, i: (off // s + b, h)),
        pl.BlockSpec((s, HEAD_W), lambda b, h, i: (off // s + b, H_C + h)),
        pl.BlockSpec((s, HEAD_W), lambda b, h, i: (off // s + b, COL_G // HEAD_W)),
        pl.BlockSpec((tq, HEAD_W), lambda b, h, i: (i, 0)),
        pl.BlockSpec((tq, HEAD_W), lambda b, h, i: (i, 0)),
        pl.BlockSpec((s, HEAD_W), lambda b, h, i: (0, 0)),
        pl.BlockSpec((s, HEAD_W), lambda b, h, i: (0, 0)),
    ]
    args = [qn, kv, kv, p, cq, sq, ck, sk]
    scratch = [pltpu.VMEM((s, HEAD_W), BF16)]
    if has_ctx:
        kvc, krc = ctx
        past = krc.shape[2]
        in_specs += [
            pl.BlockSpec((past, HEAD_W), lambda b, h, i: (b, h)),
            pl.BlockSpec((past, HEAD_W), lambda b, h, i: (b, H_C + h)),
            pl.BlockSpec((None, None, past, HEAD_W), lambda b, h, i: (b, layer, 0, 0)),
        ]
        args += [kvc, kvc, krc]
        scratch += [pltpu.VMEM((past, HEAD_W), BF16)]
    return pl.pallas_call(
        functools.partial(_attn_c_kernel, has_ctx),
        out_shape=jax.ShapeDtypeStruct((nb * s, H_C * DC_V), BF16),
        grid=(nb, H_C, s // tq),
        in_specs=in_specs,
        out_specs=pl.BlockSpec((tq, HEAD_W), lambda b, h, i: (b * (s // tq) + i, h)),
        scratch_shapes=scratch,
        compiler_params=_cparams(("parallel", "parallel", "arbitrary")),
        name="attn_c_ctx" if has_ctx else "attn_c",
    )(*args)


_C = GLA_CHUNK
_NPAIR = H_B // 2
_PAIR_K = 2 * DB_K
_PAIR_V = 2 * DB_V
_LEVELS = (1, 2, 4, 8, 16, 32)


def _bcast_row(x, blk, j):
    c, w = x.shape
    xb = x.reshape(c // blk, blk, w)
    return jnp.broadcast_to(xb[:, j:j + 1, :], xb.shape).reshape(c, w)


def _ref_rows(x, m, rev, sub):
    j = m if rev else m - 1
    if 2 * m >= 8:
        return _bcast_row(x, 2 * m, j)
    rows = [_bcast_row(x, 8, base + j) for base in range(0, 8, 2 * m)]
    out = rows[-1]
    for idx in range(len(rows) - 2, -1, -1):
        out = jnp.where(sub < (idx + 1) * 2 * m, rows[idx], out)
    return out


def _gla_dir(q, k, v, g, st_ref, rev, want_out=True):
    c, wk = q.shape
    u = lax.broadcasted_iota(jnp.int32, (c, wk), 0)
    sub = u & 7
    t_i = lax.broadcasted_iota(jnp.int32, (c, 2 * c), 0)
    s_i = lax.broadcasted_iota(jnp.int32, (c, 2 * c), 1) & (c - 1)
    lane_k = lax.broadcasted_iota(jnp.int32, (c, _PAIR_K), 1)
    lane_v = lax.broadcasted_iota(jnp.int32, (c, _PAIR_V), 1)

    x = g
    zs, masks = [], []
    for m in _LEVELS:
        qrow = ((u & m) == 0) if rev else ((u & m) != 0)
        r = _ref_rows(x, m, rev, sub)
        e = jnp.where(qrow, x, r - x)
        x = jnp.where(qrow, x + r, x)
        zs.append((jnp.where(qrow, q, k) * jnp.exp(e)).astype(BF16))
        tq = ((t_i & m) == 0) if rev else ((t_i & m) != 0)
        sk = ((s_i & m) != 0) if rev else ((s_i & m) == 0)
        masks.append((((t_i ^ s_i) & ~(2 * m - 1)) == 0) & tq & sk)
    b = x
    b_last = b[0:1, :] if rev else b[c - 1:c, :]
    qb = (q * jnp.exp(b)).astype(BF16)
    kd = (k * jnp.exp(b_last - b)).astype(BF16)
    dec = jnp.exp(b_last)
    q16, k16, v16 = q.astype(BF16), k.astype(BF16), v.astype(BF16)
    diag = t_i == s_i

    outs = []
    for p in range(_NPAIR):
        ks = slice(p * _PAIR_K, (p + 1) * _PAIR_K)
        vs = slice(p * _PAIR_V, (p + 1) * _PAIR_V)
        lo_k = lane_k < DB_K

        def stacked(z):
            zp = z[:, ks]
            zero = jnp.zeros_like(zp)
            return jnp.concatenate([jnp.where(lo_k, zp, zero), jnp.where(lo_k, zero, zp)], axis=0)

        att = jnp.where(diag, _dot_nt(q16[:, ks], stacked(k16)), 0.0)
        for z, msk in zip(zs, masks):
            att = att + jnp.where(msk, _dot_nt(z[:, ks], stacked(z)), 0.0)
        vp = v16[:, vs]
        vzero = jnp.zeros_like(vp)
        v_bd = jnp.concatenate([jnp.where(lane_v < DB_V, vp, vzero), jnp.where(lane_v < DB_V, vzero, vp)], axis=0)
        st = st_ref[p]
        if want_out:
            outs.append(_dot(att.astype(BF16), v_bd) + _dot_nt(qb[:, ks], st.astype(BF16)))
        rows = lax.broadcasted_iota(jnp.int32, (_PAIR_V, _PAIR_K), 0)
        cols = lax.broadcasted_iota(jnp.int32, (_PAIR_V, _PAIR_K), 1)
        upd = jnp.where((rows < DB_V) == (cols < DB_K), _dot_tn(vp, kd[:, ks]), 0.0)
        st_ref[p] = st * dec[:, ks] + upd
    return jnp.concatenate(outs, axis=1) if want_out else None


def _log_sigmoid(x):
    return jnp.minimum(x, 0.0) - jnp.log(1.0 + jnp.exp(-jnp.abs(x)))


def _gla_kernel(has_init, want_state, *refs):
    refs = list(refs)
    qf, kf, vf, gf, qb_, kb, vb, gb, wg_ref, bg_ref = refs[:10]
    refs = refs[10:]
    st0_ref = refs.pop(0) if has_init else None
    of_ref, ob_ref = refs[:2]
    refs = refs[2:]
    sto_ref = refs.pop(0) if want_state else None
    st_f, st_b = refs
    c = pl.program_id(1)

    @pl.when(c == 0)
    def _():
        if has_init:
            st_f[...] = st0_ref[0]
            st_b[...] = st0_ref[1]
        else:
            st_f[...] = jnp.zeros_like(st_f)
            st_b[...] = jnp.zeros_like(st_b)

    wk = H_B * DB_K
    scale = DB_K ** -0.5
    pre_f = _dot(gf[...].astype(BF16), wg_ref[:, :wk]) + bg_ref[:, :wk]
    pre_b = _dot(gb[...].astype(BF16), wg_ref[:, wk:]) + bg_ref[:, wk:]
    of_ref[...] = _gla_dir(qf[...] * scale, kf[...], vf[...], _log_sigmoid(pre_f) / GATE_TAU, st_f, False)
    ob_ref[...] = _gla_dir(qb_[...] * scale, kb[...], vb[...], _log_sigmoid(pre_b) / GATE_TAU, st_b, True)

    if want_state:
        @pl.when(c == pl.num_programs(1) - 1)
        def _():
            sto_ref[0] = st_f[...]
            sto_ref[1] = st_b[...]


def _gla(p, wg, bg, grp, st0, want_state):
    off, nb, s = grp
    nc = s // _C
    wk, wv = H_B * DB_K, H_B * DB_V
    base = off // _C

    def specs(cidx):
        row = lambda b, c: base + b * nc + cidx(c)
        return [
            pl.BlockSpec((_C, wk), lambda b, c: (row(b, c), COL_BQ // wk)),
            pl.BlockSpec((_C, wk), lambda b, c: (row(b, c), COL_BK // wk)),
            pl.BlockSpec((_C, wv), lambda b, c: (row(b, c), COL_BV // wv)),
            pl.BlockSpec((_C, HEAD_W), lambda b, c: (row(b, c), COL_G // HEAD_W)),
        ]

    fwd = lambda c: c
    bwd = lambda c: nc - 1 - c
    in_specs = specs(fwd) + specs(bwd) + [
        pl.BlockSpec((HEAD_W, 2 * wk), lambda b, c: (0, 0)),
        pl.BlockSpec((1, 2 * wk), lambda b, c: (0, 0)),
    ]
    args = [p] * 8 + [wg, bg]
    has_init = st0 is not None
    if has_init:
        in_specs.append(pl.BlockSpec((None, 2, _NPAIR, _PAIR_V, _PAIR_K), lambda b, c: (b, 0, 0, 0, 0)))
        args.append(st0)
    out_shape = [jax.ShapeDtypeStruct((nb * s, wv), F32), jax.ShapeDtypeStruct((nb * s, wv), F32)]
    out_specs = [
        pl.BlockSpec((_C, wv), lambda b, c: (b * nc + c, 0)),
        pl.BlockSpec((_C, wv), lambda b, c: (b * nc + nc - 1 - c, 0)),
    ]
    if want_state:
        out_shape.append(jax.ShapeDtypeStruct((nb, 2, _NPAIR, _PAIR_V, _PAIR_K), F32))
        out_specs.append(pl.BlockSpec((None, 2, _NPAIR, _PAIR_V, _PAIR_K), lambda b, c: (b, 0, 0, 0, 0)))
    return pl.pallas_call(
        functools.partial(_gla_kernel, has_init, want_state),
        out_shape=tuple(out_shape),
        grid=(nb, nc),
        in_specs=in_specs,
        out_specs=tuple(out_specs),
        scratch_shapes=[pltpu.VMEM((_NPAIR, _PAIR_V, _PAIR_K), F32), pltpu.VMEM((_NPAIR, _PAIR_V, _PAIR_K), F32)],
        compiler_params=_cparams(("parallel", "arbitrary")),
        name="gla_ctx" if has_init else "gla",
    )(*args)


def _gla_fin_kernel(of_ref, ob_ref, r_ref, g_ref, o_ref):
    o = of_ref[...] + ob_ref[...]
    r = r_ref[...]
    g = g_ref[...]
    for h in range(H_B):
        sl = slice(h * DB_V, (h + 1) * DB_V)
        o_ref[:, sl] = (_rms(o[:, sl], g) * _silu(r[:, sl])).astype(o_ref.dtype)


def _gla_fin(of, ob, p, g, off, tm):
    t, wv = of.shape
    return pl.pallas_call(
        _gla_fin_kernel,
        out_shape=jax.ShapeDtypeStruct((t, wv), BF16),
        grid=(t // tm,),
        in_specs=[
            pl.BlockSpec((tm, wv), lambda i: (i, 0)),
            pl.BlockSpec((tm, wv), lambda i: (i, 0)),
            pl.BlockSpec((tm, wv), lambda i: (off // tm + i, COL_BR // wv)),
            pl.BlockSpec((1, DB_V), lambda i: (0, 0)),
        ],
        out_specs=pl.BlockSpec((tm, wv), lambda i: (i, 0)),
        compiler_params=_cparams(("parallel",)),
        name="gla_fin",
    )(of, ob, p, g)


def _proj_out_kernel(a_ref, b_ref, c_ref, wa_ref, wb_ref, wc_ref, x_ref, g1_ref, o_ref):
    acc = _dot(a_ref[...], wa_ref[...]) + _dot(b_ref[...], wb_ref[...]) + _dot(c_ref[...], wc_ref[...])
    o_ref[...] = x_ref[...] + g1_ref[...] * acc


def _proj_out(a, b, c, w, x, mod5, layer, row_fn, tm, tn):
    t, d = x.shape
    wa_rows, wb_rows, wc_rows = a.shape[1], b.shape[1], c.shape[1]
    return pl.pallas_call(
        _proj_out_kernel,
        out_shape=jax.ShapeDtypeStruct((t, d), F32),
        grid=(t // tm, d // tn),
        in_specs=[
            pl.BlockSpec((tm, wa_rows), lambda i, j: (i, 0)),
            pl.BlockSpec((tm, wb_rows), lambda i, j: (i, 0)),
            pl.BlockSpec((tm, wc_rows), lambda i, j: (i, 0)),
            pl.BlockSpec((wa_rows, tn), lambda i, j: (0, j)),
            pl.BlockSpec((wb_rows, tn), lambda i, j: (0, j)),
            pl.BlockSpec((wc_rows, tn), lambda i, j: (0, j)),
            pl.BlockSpec((tm, tn), lambda i, j: (i, j)),
            pl.BlockSpec((None, None, None, 1, tn), lambda i, j: (layer, row_fn(i), 2, 0, j)),
        ],
        out_specs=pl.BlockSpec((tm, tn), lambda i, j: (i, j)),
        compiler_params=_cparams(("parallel", "parallel")),
        name="proj_out",
    )(a, b, c, w[:wa_rows], w[wa_rows:wa_rows + wb_rows], w[wa_rows + wb_rows:], x, mod5)


HALO = 8


def _ffn_kernel(seq_info, x_ref, xp_ref, xn_ref, g_ref, sc_ref, sh_ref, g2_ref, wgu_ref, cw_ref, cb_ref,
                wd_ref, o_ref, h_ref, acc_ref):
    n_prompt, s_prompt, s_lat = seq_info
    f = pl.program_id(1)
    tm = x_ref.shape[0]
    tf = cw_ref.shape[1]

    @pl.when(f == 0)
    def _():
        g, sc, sh = g_ref[...], sc_ref[...], sh_ref[...]
        mod = lambda xx: (_rms(xx, g) * (1.0 + sc) + sh).astype(BF16)
        h_ref[0:HALO, :] = mod(xp_ref[...])
        h_ref[HALO:HALO + tm, :] = mod(x_ref[...])
        h_ref[HALO + tm:, :] = mod(xn_ref[...])
        acc_ref[...] = jnp.zeros_like(acc_ref)

    gu = _dot(h_ref[...], wgu_ref[...])
    ge = gu[:, :tf]
    u = gu[HALO:HALO + tm, tf:]
    rows = tm + 2 * HALO
    g_prev = pltpu.roll(ge, 1, 0)[HALO:HALO + tm]
    g_next = pltpu.roll(ge, rows - 1, 0)[HALO:HALO + tm]
    tok = pl.program_id(0) * tm + lax.broadcasted_iota(jnp.int32, (tm, 1), 0)
    pos = jnp.where(tok < n_prompt, tok & (s_prompt - 1), (tok - n_prompt) & (s_lat - 1))
    slen = jnp.where(tok < n_prompt, s_prompt, s_lat)
    not_first = (pos != 0).astype(F32)
    not_last = (pos != slen - 1).astype(F32)
    cw = cw_ref[...]
    gc = (cw[0:1] * (g_prev * not_first) + cw[1:2] * ge[HALO:HALO + tm] + cw[2:3] * (g_next * not_last)
          + cb_ref[...])
    act = (_silu(gc) * u).astype(BF16)
    acc_ref[...] += _dot(act, wd_ref[...])

    @pl.when(f == pl.num_programs(1) - 1)
    def _():
        o_ref[...] = x_ref[...] + g2_ref[...] * acc_ref[...]


def _ffn(x, g, mod5, wgu, cw, cb, wd, layer, row_fn, seq_info, tm, tf):
    t, d = x.shape
    nf = wd.shape[0] // tf
    nblk = t // HALO

    def mspec(which):
        return pl.BlockSpec((None, None, None, 1, d), lambda i, f: (layer, row_fn(i), which, 0, 0))

    return pl.pallas_call(
        functools.partial(_ffn_kernel, seq_info),
        out_shape=jax.ShapeDtypeStruct((t, d), F32),
        grid=(t // tm, nf),
        in_specs=[
            pl.BlockSpec((tm, d), lambda i, f: (i, 0)),
            pl.BlockSpec((HALO, d), lambda i, f: (jnp.maximum(i * (tm // HALO) - 1, 0), 0)),
            pl.BlockSpec((HALO, d), lambda i, f: (jnp.minimum((i + 1) * (tm // HALO), nblk - 1), 0)),
            pl.BlockSpec((1, d), lambda i, f: (0, 0)),
            mspec(4), mspec(3), mspec(5),
            pl.BlockSpec((d, 2 * tf), lambda i, f: (0, f)),
            pl.BlockSpec((3, tf), lambda i, f: (0, f)),
            pl.BlockSpec((1, tf), lambda i, f: (0, f)),
            pl.BlockSpec((tf, d), lambda i, f: (f, 0)),
        ],
        out_specs=pl.BlockSpec((tm, d), lambda i, f: (i, 0)),
        scratch_shapes=[pltpu.VMEM((tm + 2 * HALO, d), BF16), pltpu.VMEM((tm, d), F32)],
        compiler_params=_cparams(("parallel", "arbitrary")),
        name="conv_ffn",
    )(x, x, x, g, mod5, mod5, mod5, wgu, cw, cb, wd)


def _final_kernel(x_ref, g_ref, o_ref):
    o_ref[...] = _rms(x_ref[...], g_ref[...])


def _final_norm(x, g, off, n, tm):
    d = x.shape[1]
    return pl.pallas_call(
        _final_kernel,
        out_shape=jax.ShapeDtypeStruct((n, d), F32),
        grid=(n // tm,),
        in_specs=[pl.BlockSpec((tm, d), lambda i: (off // tm + i, 0)), pl.BlockSpec((1, d), lambda i: (0, 0))],
        out_specs=pl.BlockSpec((tm, d), lambda i: (i, 0)),
        compiler_params=_cparams(("parallel",)),
        name="final_norm",
    )(x, g)


def _rope_angles(n_rows, rot_dim):
    pos_r = jnp.repeat(jnp.arange(n_rows, dtype=F32), GRID_W)
    pos_c = jnp.tile(jnp.arange(GRID_W, dtype=F32), n_rows)
    n_freq = rot_dim // 4
    inv = ROPE_BASE ** (-jnp.arange(n_freq, dtype=F32) / n_freq)
    a_r = pos_r[:, None] * inv[None]
    a_c = pos_c[:, None] * inv[None]
    ang = jnp.concatenate([a_r, a_r, a_c, a_c], axis=-1)
    sign = jnp.concatenate([-jnp.ones(n_freq), jnp.ones(n_freq), -jnp.ones(n_freq), jnp.ones(n_freq)]).astype(F32)
    return jnp.cos(ang), jnp.sin(ang) * sign[None]


def _tables_a(s, rotate):
    if rotate:
        cos, sin = _rope_angles(s // GRID_W, DA_QK)
        return jnp.tile(cos, (1, 2)), jnp.tile(sin, (1, 2))
    return jnp.ones((s, HEAD_W), F32), jnp.zeros((s, HEAD_W), F32)


def _tables_c(s, rotate):
    if rotate:
        cos, sin = _rope_angles(s // GRID_W, DC_ROPE)
    else:
        cos, sin = jnp.ones((s, DC_ROPE), F32), jnp.zeros((s, DC_ROPE), F32)
    pad = lambda t, v: jnp.concatenate(
        [jnp.full((s, KR_LANE), v, F32), t, jnp.full((s, HEAD_W - KR_LANE - DC_ROPE), v, F32)], axis=1)
    return pad(cos, 1.0), pad(sin, 0.0), pad(cos, 0.0), pad(sin, 0.0)


def _layout_w_in(w):
    z = jnp.zeros(w.shape[:-1] + (32,), w.dtype)
    o_bg, o_cq, o_ckv, o_ckr = 3840, 3872, 4256, 4512
    g_block = jnp.concatenate([w[..., o_bg:o_bg + 32], z, w[..., o_ckr:o_ckr + 32], z], axis=-1)
    out = jnp.concatenate([w[..., :o_bg], w[..., o_cq:o_ckv], g_block, w[..., o_ckv:o_ckr]], axis=-1)
    return out.astype(BF16)


def _layout_gate(b_gate_w, b_gate_b):
    depth = b_gate_w.shape[0]
    wk = H_B * DB_K
    wg = jnp.zeros((depth, HEAD_W, 2 * wk), F32)
    wg = wg.at[:, :GATE_RANK, :wk].set(b_gate_w[:, 0]).at[:, GATE_RANK:2 * GATE_RANK, wk:].set(b_gate_w[:, 1])
    return wg.astype(BF16), b_gate_b.reshape(depth, 1, 2 * wk)


def _layout_q_up(w):
    depth = w.shape[0]
    w = w.reshape(depth, Q_RANK, H_C, DC_NOPE + DC_ROPE)
    w = jnp.pad(w, ((0, 0), (0, 0), (0, 0), (0, HEAD_W - DC_NOPE - DC_ROPE)))
    return w.reshape(depth, Q_RANK, H_C * HEAD_W).astype(BF16)


def _layout_kv_up(w):
    depth = w.shape[0]
    w = w.reshape(depth, KV_RANK, H_C, DC_NOPE + DC_V)
    wk = jnp.pad(w[..., :DC_NOPE], ((0, 0), (0, 0), (0, 0), (0, HEAD_W - DC_NOPE)))
    wv = w[..., DC_NOPE:]
    return jnp.concatenate([wk.reshape(depth, KV_RANK, -1), wv.reshape(depth, KV_RANK, -1)], axis=-1).astype(BF16)


def _layout_w_up(w, tf):
    depth, d, n2 = w.shape
    dff = n2 // 2
    g = w[..., :dff].reshape(depth, d, dff // tf, tf)
    u = w[..., dff:].reshape(depth, d, dff // tf, tf)
    return jnp.concatenate([g, u], axis=-1).reshape(depth, d, n2).astype(BF16)


def _state_to_pairs(s):
    st = jnp.swapaxes(s, -1, -2)
    nb = s.shape[0]
    st = st.reshape(nb, 2, _NPAIR, 2, DB_V, DB_K)
    z = jnp.zeros_like(st[:, :, :, 0])
    top = jnp.concatenate([st[:, :, :, 0], z], axis=-1)
    bot = jnp.concatenate([z, st[:, :, :, 1]], axis=-1)
    return jnp.concatenate([top, bot], axis=-2)


def _pairs_to_state(st):
    nb = st.shape[0]
    a = st[:, :, :, :DB_V, :DB_K]
    b = st[:, :, :, DB_V:, DB_K:]
    s = jnp.stack([a, b], axis=3).reshape(nb, 2, H_B, DB_V, DB_K)
    return jnp.swapaxes(s, -1, -2)


def kernel(x_prompt, x_sample, cache_a_k, cache_a_v, state_b, cache_c_kv, cache_c_krope, c, c_ctx, norm1_g, ada_w, ada_b, w_in, lam_q1, lam_k1, lam_q2, lam_k2, a_norm_g, b_gate_w, b_gate_b, b_norm_g, c_qnorm_g, c_q_up, c_kvnorm_g, c_kv_up, w_out, norm2_g, w_up, conv_w, conv_b, w_down, final_g):
    bp, sp, d = x_prompt.shape
    bl, sl, _ = x_sample.shape
    depth = w_in.shape[0]
    past = cache_a_k.shape[2]
    n_p, n_l = bp * sp, bl * sl
    assert n_p % sl == 0 or sl % n_p == 0
    assert bl + 1 <= 8 and sl % GRID_W == 0
    tm = min(512, sl, n_p)
    tf = 512
    row_fn = lambda i: jnp.where(i * tm < n_p, 0, 1 + (i * tm - n_p) // sl)
    grp_p, grp_l = (0, bp, sp), (n_p, bl, sl)

    x = jnp.concatenate([x_prompt.reshape(n_p, d), x_sample.reshape(n_l, d)], axis=0)
    cond8 = jnp.concatenate([c_ctx[None], c, jnp.zeros((8 - 1 - bl, d), F32)], axis=0)
    mod5 = _ada(cond8, ada_w, ada_b).reshape(depth, 8, 6, 1, d)

    w_in_l = _layout_w_in(w_in)
    wg_l, bg_l = _layout_gate(b_gate_w, b_gate_b)
    wq_l = _layout_q_up(c_q_up)
    wkv_l = _layout_kv_up(c_kv_up)
    w_out_l = w_out.astype(BF16)
    w_up_l = _layout_w_up(w_up, tf)
    w_down_l = w_down.astype(BF16)
    lam4 = jnp.stack([lam_q1, lam_k1, lam_q2, lam_k2], axis=1)

    tab_a_p, tab_a_l = _tables_a(sp, False), _tables_a(sl, True)
    tab_c_p, tab_c_l = _tables_c(sp, False), _tables_c(sl, True)
    ck_view = cache_a_k.reshape(bl, depth, past, H_A * 2 * DA_QK)
    cv_view = cache_a_v.reshape(bl, depth, past, H_A * DA_V)
    krc = jnp.pad(cache_c_krope, ((0, 0), (0, 0), (0, 0), (KR_LANE, HEAD_W - KR_LANE - DC_ROPE)))
    st0 = _state_to_pairs(state_b.reshape((bl * depth,) + state_b.shape[2:])).reshape(
        (bl, depth, 2, _NPAIR, _PAIR_V, _PAIR_K))

    ak_l, av_l, sb_l, ckv_l, kr_l = [], [], [], [], []
    for l in range(depth):
        p = _proj_in(x, norm1_g[l][None], mod5, w_in_l[l], l, row_fn, tm, 512)
        qn = _mla_q(p, c_qnorm_g[l][None], wq_l[l], tm)
        ckv, kv = _mla_kv(p, c_kvnorm_g[l][None], wkv_l[l], tm)
        kvc = _mla_kvc(cache_c_kv[:, l].reshape(bl * past, KV_RANK), wkv_l[l], past)

        a_p = _attn_a(p, tab_a_p, lam4[l], a_norm_g[l][None], l, grp_p, None)
        a_l = _attn_a(p, tab_a_l, lam4[l], a_norm_g[l][None], l, grp_l, (ck_view, cv_view))
        c_p = _attn_c(qn, kv, p, tab_c_p, l, grp_p, None)
        c_l = _attn_c(qn, kv, p, tab_c_l, l, grp_l, (kvc, krc))
        of_p, ob_p, st_p = _gla(p, wg_l[l], bg_l[l], grp_p, None, True)
        of_l, ob_l = _gla(p, wg_l[l], bg_l[l], grp_l, st0[:, l], False)

        b_mix = _gla_fin(jnp.concatenate([of_p, of_l], axis=0), jnp.concatenate([ob_p, ob_l], axis=0), p,
                         b_norm_g[l][None], 0, tm)
        a_mix = jnp.concatenate([a_p, a_l], axis=0)
        c_mix = jnp.concatenate([c_p, c_l], axis=0)
        x = _proj_out(a_mix, b_mix, c_mix, w_out_l[l], x, mod5, l, row_fn, tm, 1024)
        x = _ffn(x, norm2_g[l][None], mod5, w_up_l[l], conv_w[l], conv_b[l][None], w_down_l[l], l, row_fn,
                 (n_p, sp, sl), tm, tf)

        ak_l.append(p[:n_p, COL_AK:COL_AK + 2 * H_A * DA_QK].reshape(bp, sp, H_A, 2, DA_QK))
        av_l.append(p[:n_p, COL_AV:COL_AV + H_A * DA_V].reshape(bp, sp, H_A, DA_V))
        sb_l.append(_pairs_to_state(st_p))
        ckv_l.append(ckv[:n_p].reshape(bp, sp, KV_RANK))
        kr_l.append(p[:n_p, COL_G + KR_LANE:COL_G + KR_LANE + DC_ROPE].reshape(bp, sp, DC_ROPE))

    y_prompt = _final_norm(x, final_g[None], 0, n_p, tm).reshape(bp, sp, d)
    y_sample = _final_norm(x, final_g[None], n_p, n_l, tm).reshape(bl, sl, d)
    return (y_prompt, y_sample, jnp.stack(ak_l, axis=1), jnp.stack(av_l, axis=1), jnp.stack(sb_l, axis=1),
            jnp.stack(ckv_l, axis=1), jnp.stack(kr_l, axis=1))
```
